```python
import math
import jax
import jax.numpy as jnp
from jax import lax
import numpy as np

D_MODEL = 2048
BATCH = 4
SEQ = 2048
DEPTH = 4
DEC_BATCH = 8
DEC_SEQ = 8
PAST_LEN = 16384
PAGE_SIZE = 128

F32 = jnp.float32
RW_WIDTH = D_MODEL // 2
RW_HD = 64
RW_HEADS = RW_WIDTH // RW_HD
RW_LORA_W = 64
RW_LORA_A = 64
RW_LORA_G = 160
RW_COLS = 3 * RW_WIDTH + RW_LORA_W + RW_LORA_A + RW_LORA_G
RW_GN_EPS = 1e-5 * RW_HD
HG_WIDTH = D_MODEL // 2
HG_DK = 128
HG_DV = 128
HG_HEADS = HG_WIDTH // HG_DK
HG_CHUNK = 32
HG_COLS = 4 * HG_WIDTH
HG_EPS = 1e-5
FX_WIDTH = D_MODEL // 2
FX_HD = 128
FX_HEADS = FX_WIDTH // FX_HD
FX_QBLOCK = 128
FX_COLS = 3 * FX_WIDTH + FX_HEADS
FX_SCALE = FX_HD ** -0.5
FX_BF_INIT = 4.0
N_BRANCH = 3
GATE_COLS = N_BRANCH * D_MODEL
N_IN = RW_COLS + HG_COLS + FX_COLS + GATE_COLS
N_EXPERTS = 32
TOP_K = 4
D_FF = D_MODEL
SWIGLU_LIMIT = 7.0
SWIGLU_ALPHA = 1.702
MOE_BLOCK = 128
DN_ALPHA = (2 * DEPTH) ** 0.25
DN_BETA = (8 * DEPTH) ** -0.25
LN_EPS = 1e-5

kernel_name = 'hybrid_rwkv7_hgrn2_fox_moe_step'


def _split(t, sizes):
    return jnp.split(t, np.cumsum(sizes)[:-1].tolist(), axis=-1)


def layer_norm(x, g, b):
    xf = x.astype(F32)
    mu = xf.mean(-1, keepdims=True)
    var = jnp.square(xf - mu).mean(-1, keepdims=True)
    return ((xf - mu) * lax.rsqrt(var + LN_EPS) * g + b).astype(x.dtype)


def rwkv7_branch(p, shift_prev, s0, lp):
    B, L, _ = p.shape
    p_prev = jnp.concatenate([shift_prev[:, None, :].astype(p.dtype), p[:, :-1]], axis=1)
    xs = p + lp['rw_mu'] * (p_prev - p)
    r, k, v, lw, la, lg = _split(xs, [RW_WIDTH] * 3 + [RW_LORA_W, RW_LORA_A, RW_LORA_G])
    w_log = -jax.nn.softplus(-(lp['rw_w0'] + jnp.tanh(lw) @ lp['rw_w_up'])) - 0.5
    decay = jnp.exp(-jnp.exp(w_log.astype(F32)))
    a = jax.nn.sigmoid(lp['rw_a0'] + la @ lp['rw_a_up'])
    g = jax.nn.sigmoid(lg) @ lp['rw_g_up']
    hd = lambda t: t.reshape(B, L, RW_HEADS, RW_HD).astype(F32)
    kk = hd(k * lp['rw_k_k'])
    kk = kk / jnp.maximum(jnp.linalg.norm(kk, axis=-1, keepdims=True), 1e-12)
    k = k * (1 + (a - 1) * lp['rw_k_a'])
    r_h, k_h, v_h, w_h, a_h = hd(r), hd(k), hd(v), hd(decay), hd(a)

    def step(S, inp):
        r_t, w_t, k_t, v_t, kk_t, a_t = inp
        sa = jnp.einsum('bhvk,bhk->bhv', S, kk_t)
        S = (S * w_t[:, :, None, :] - sa[..., None] * (kk_t * a_t)[:, :, None, :]
             + v_t[..., None] * k_t[:, :, None, :])
        return S, jnp.einsum('bhvk,bhk->bhv', S, r_t)

    seq = tuple(jnp.moveaxis(t, 1, 0) for t in (r_h, w_h, k_h, v_h, kk, a_h))
    s_fin, y = lax.scan(step, s0.astype(F32), seq)
    y = jnp.moveaxis(y, 0, 1)
    mu = y.mean(-1, keepdims=True)
    var = jnp.square(y - mu).mean(-1, keepdims=True)
    yn = ((y - mu) * lax.rsqrt(var + RW_GN_EPS)).reshape(B, L, RW_WIDTH) * lp['rw_gn_g'] + lp['rw_gn_b']
    bonus = jnp.sum(r_h * k_h * lp['rw_r_k'], -1, keepdims=True) * v_h
    out = ((yn + bonus.reshape(B, L, RW_WIDTH)) * g).astype(p.dtype)
    return out, p[:, -1], s_fin


def _to_chunks(t, c):
    b, l, h, d = t.shape
    pad = (-l) % c
    t = jnp.pad(t, ((0, 0), (0, pad), (0, 0), (0, 0)))
    return t.reshape(b, (l + pad) // c, c, h, d).transpose(1, 0, 3, 2, 4)


def hgrn2_branch(hq, hf, hi, hgz, s0, lb, norm_g):
    B, L, _ = hq.shape
    lb = jnp.maximum(lb.astype(F32), 0.0)
    z = hf.astype(F32)
    log_f = jnp.logaddexp(jnp.log(lb), jnp.log1p(-lb) + jax.nn.log_sigmoid(z))
    k = (1.0 - lb) * jax.nn.sigmoid(-z)
    q = jax.nn.silu(hq.astype(F32))
    hd = lambda t, d: t.reshape(B, L, HG_HEADS, d)
    chunks = (_to_chunks(hd(q, HG_DK), HG_CHUNK), _to_chunks(hd(k, HG_DK), HG_CHUNK),
              _to_chunks(hd(hi.astype(F32), HG_DV), HG_CHUNK), _to_chunks(hd(log_f, HG_DK), HG_CHUNK))
    causal = jnp.tril(jnp.ones((HG_CHUNK, HG_CHUNK), dtype=bool))[:, :, None]

    def chunk_step(S, inp):
        q_c, k_c, i_c, lf_c = inp
        b = jnp.cumsum(lf_c, axis=2)
        o_inter = jnp.einsum('bhtk,bhkv->bhtv', q_c * jnp.exp(b), S)
        diff = b[:, :, :, None, :] - b[:, :, None, :, :]
        dec = jnp.exp(jnp.where(causal, diff, -jnp.inf))
        A = jnp.einsum('bhtk,bhsk,bhtsk->bhts', q_c, k_c, dec)
        o = o_inter + jnp.einsum('bhts,bhsv->bhtv', A, i_c)
        b_last = b[:, :, -1:, :]
        S = (jnp.exp(b_last[:, :, 0, :])[..., None] * S
             + jnp.einsum('bhsk,bhsv->bhkv', k_c * jnp.exp(b_last - b), i_c))
        return S, o

    s_fin, o = lax.scan(chunk_step, s0.astype(F32), chunks)
    o = o.transpose(1, 0, 3, 2, 4).reshape(B, -1, HG_HEADS, HG_DV)[:, :L]
    o = o * lax.rsqrt(jnp.mean(jnp.square(o), -1, keepdims=True) + HG_EPS) * norm_g
    o = o.reshape(B, L, HG_WIDTH) * jax.nn.silu(hgz.astype(F32))
    return o.astype(hq.dtype), s_fin


def fox_attention(q, k, v, c_q, c_k, q_pos, k_pos):
    B, Lq, H, Dh = q.shape
    blk = FX_QBLOCK if Lq % FX_QBLOCK == 0 else Lq
    nb = Lq // blk
    blocks = lambda t: jnp.moveaxis(t.reshape((B, nb, blk) + t.shape[2:]), 1, 0)
    ck_t = jnp.swapaxes(c_k, 1, 2)

    def one_block(args):
        q_b, cq_b, pos_b = args
        s = jnp.einsum('bqhd,bkhd->bhqk', q_b, k).astype(F32) * FX_SCALE
        s = s + jnp.swapaxes(cq_b, 1, 2)[..., None] - ck_t[:, :, None, :]
        s = jnp.where((k_pos[None, :] <= pos_b[:, None])[None, None], s, -jnp.inf)
        p = jax.nn.softmax(s, axis=-1)
        return jnp.einsum('bhqk,bkhd->bqhd', p.astype(v.dtype), v)

    o = lax.map(one_block, (blocks(q), blocks(c_q), q_pos.reshape(nb, blk)))
    return jnp.moveaxis(o, 0, 1).reshape(B, Lq, H * Dh)


def token_mixer(x, lp, shift_prev, s_rw, s_hg, past):
    B, L, _ = x.shape
    proj = x @ lp['w_in']
    p_rw, p_hg, p_fx, p_gate = _split(proj, [RW_COLS, HG_COLS, FX_COLS, GATE_COLS])
    o_rw, new_shift, new_srw = rwkv7_branch(p_rw, shift_prev, s_rw, lp)
    hq, hf, hi, hgz = _split(p_hg, [HG_WIDTH] * 4)
    o_hg, new_shg = hgrn2_branch(hq, hf, hi, hgz, s_hg, lp['hg_lb'], lp['hg_norm_g'])
    fq, fk, fv, fz = _split(p_fx, [FX_WIDTH] * 3 + [FX_HEADS])
    hd = lambda t: t.reshape(B, L, FX_HEADS, FX_HD)
    fq, fk, fv = hd(fq), hd(fk), hd(fv)
    lf = jax.nn.log_sigmoid(fz.astype(F32) + lp['fx_bf'])
    c_new = jnp.cumsum(lf, axis=1)
    if past is None:
        k_all, v_all, c_k = fk, fv, c_new
        q_pos = jnp.arange(L)
        k_pos = q_pos
    else:
        k_past, v_past, lf_past = past
        P = k_past.shape[1]
        c_past = -(lax.cumsum(lf_past, axis=1, reverse=True) - lf_past)
        k_all = jnp.concatenate([k_past.astype(fk.dtype), fk], axis=1)
        v_all = jnp.concatenate([v_past.astype(fv.dtype), fv], axis=1)
        c_k = jnp.concatenate([c_past, c_new], axis=1)
        q_pos = P + jnp.arange(L)
        k_pos = jnp.arange(P + L)
    o_fx = fox_attention(fq, k_all, v_all, c_new, c_k, q_pos, k_pos)
    gate = jax.nn.sigmoid(p_gate).reshape(B, L, N_BRANCH, D_MODEL)
    merged = (gate[:, :, 0] * (o_rw @ lp['p_rw']) + gate[:, :, 1] * (o_hg @ lp['p_hg'])
              + gate[:, :, 2] * (o_fx @ lp['p_fx']))
    y = merged @ lp['w_o']
    return y, (fk, fv, lf, new_srw, new_shift, new_shg)


def moe_ffn(x, lp):
    B, L, D = x.shape
    T = B * L
    xt = x.reshape(T, D)
    logits = (xt @ lp['w_router'] + lp['b_router']).astype(F32)
    top_vals, top_idx = lax.top_k(logits, TOP_K)
    gates = jax.nn.softmax(top_vals, axis=-1)
    TK = T * TOP_K
    flat_e = top_idx.reshape(TK)
    flat_tok = jnp.arange(TK, dtype=jnp.int32) // TOP_K
    flat_gate = gates.reshape(TK)
    order = jnp.argsort(flat_e)
    e_sorted = flat_e[order]
    counts = jnp.bincount(flat_e, length=N_EXPERTS)
    starts = jnp.cumsum(counts) - counts
    padded = (counts + MOE_BLOCK - 1) // MOE_BLOCK * MOE_BLOCK
    pends = jnp.cumsum(padded)
    pstarts = pends - padded
    dest = pstarts[e_sorted] + (jnp.arange(TK, dtype=jnp.int32) - starts[e_sorted])
    n_blocks = -(-(TK + N_EXPERTS * (MOE_BLOCK - 1)) // MOE_BLOCK)
    n_rows = n_blocks * MOE_BLOCK
    row_tok = jnp.full((n_rows,), T, jnp.int32).at[dest].set(flat_tok[order])
    row_gate = jnp.zeros((n_rows,), F32).at[dest].set(flat_gate[order])
    block_e = jnp.minimum(jnp.searchsorted(pends, jnp.arange(n_blocks) * MOE_BLOCK, side='right'), N_EXPERTS - 1)
    x_pad = jnp.concatenate([xt, jnp.zeros((1, D), xt.dtype)], axis=0)
    w_up, b_up, w_down, b_down = lp['w_up'], lp['b_up'], lp['w_down'], lp['b_down']

    def expert_block(args):
        tok_b, e = args
        h = x_pad[tok_b] @ w_up[e] + b_up[e]
        glu = jnp.minimum(h[:, ::2], SWIGLU_LIMIT)
        lin = jnp.clip(h[:, 1::2], -SWIGLU_LIMIT, SWIGLU_LIMIT)
        act = glu * jax.nn.sigmoid(SWIGLU_ALPHA * glu) * (lin + 1)
        return act @ w_down[e] + b_down[e]

    y_rows = lax.map(expert_block, (row_tok.reshape(n_blocks, MOE_BLOCK), block_e))
    y_rows = y_rows.reshape(n_rows, D) * row_gate[:, None]
    y = jax.ops.segment_sum(y_rows, row_tok, num_segments=T + 1)[:T]
    return y.reshape(B, L, D).astype(x.dtype)


def setup_inputs(seed: int = 0) -> dict:
    key = jax.random.key(seed)
    keys = iter(jax.random.split(key, 64))

    def nrm(shape, scale=1.0):
        return scale * jax.random.normal(next(keys), shape, F32)

    n_pages = PAST_LEN // PAGE_SIZE
    n_used = DEC_BATCH * n_pages
    n_pool = n_used + max(1, n_used // 4)
    page_table = jax.random.permutation(next(keys), n_pool)[:n_used].reshape(DEC_BATCH, n_pages).astype(jnp.int32)
    d = D_MODEL
    return {
        'x_prompt': nrm((BATCH, SEQ, d)),
        'x_sample': nrm((DEC_BATCH, DEC_SEQ, d)),
        'cache_k': nrm((DEPTH, n_pool, PAGE_SIZE, FX_HEADS, FX_HD)),
        'cache_v': nrm((DEPTH, n_pool, PAGE_SIZE, FX_HEADS, FX_HD)),
        'cache_logf': jax.nn.log_sigmoid(FX_BF_INIT + nrm((DEPTH, n_pool, PAGE_SIZE, FX_HEADS))),
        'state_rwkv': nrm((DEPTH, DEC_BATCH, RW_HEADS, RW_HD, RW_HD)),
        'state_shift': nrm((DEPTH, DEC_BATCH, RW_COLS)),
        'state_hgrn': nrm((DEPTH, DEC_BATCH, HG_HEADS, HG_DK, HG_DV), 0.5),
        'page_table': page_table,
        'w_in': nrm((DEPTH, d, N_IN), d ** -0.5),
        'rw_mu': jax.random.uniform(next(keys), (DEPTH, RW_COLS), F32),
        'rw_w0': nrm((DEPTH, RW_WIDTH), 0.5),
        'rw_w_up': nrm((DEPTH, RW_LORA_W, RW_WIDTH), 0.1),
        'rw_a0': nrm((DEPTH, RW_WIDTH), 0.5),
        'rw_a_up': nrm((DEPTH, RW_LORA_A, RW_WIDTH), 0.1),
        'rw_g_up': nrm((DEPTH, RW_LORA_G, RW_WIDTH), 2.0 * RW_LORA_G ** -0.5),
        'rw_k_k': 0.85 + nrm((DEPTH, RW_WIDTH), 0.05),
        'rw_k_a': 1.0 + nrm((DEPTH, RW_WIDTH), 0.05),
        'rw_r_k': nrm((DEPTH, RW_HEADS, RW_HD), 0.1),
        'rw_gn_g': 1.0 + nrm((DEPTH, RW_WIDTH), 0.1),
        'rw_gn_b': nrm((DEPTH, RW_WIDTH), 0.01),
        'hg_gamma': nrm((DEPTH, HG_WIDTH)),
        'hg_norm_g': 1.0 + nrm((DEPTH, HG_DV), 0.1),
        'fx_bf': FX_BF_INIT + nrm((DEPTH, FX_HEADS), 0.5),
        'p_rw': nrm((DEPTH, RW_WIDTH, d), RW_WIDTH ** -0.5),
        'p_hg': nrm((DEPTH, HG_WIDTH, d), HG_WIDTH ** -0.5),
        'p_fx': nrm((DEPTH, FX_WIDTH, d), FX_WIDTH ** -0.5),
        'w_o': nrm((DEPTH, d, d), DN_BETA * d ** -0.5),
        'ln1_g': 1.0 + nrm((DEPTH, d), 0.1),
        'ln1_b': nrm((DEPTH, d), 0.01),
        'ln2_g': 1.0 + nrm((DEPTH, d), 0.1),
        'ln2_b': nrm((DEPTH, d), 0.01),
        'w_router': nrm((DEPTH, d, N_EXPERTS), d ** -0.5),
        'b_router': nrm((DEPTH, N_EXPERTS), 0.01),
        'w_up': nrm((DEPTH, N_EXPERTS, d, 2 * D_FF), d ** -0.5),
        'b_up': nrm((DEPTH, N_EXPERTS, 2 * D_FF), 0.01),
        'w_down': nrm((DEPTH, N_EXPERTS, D_FF, d), DN_BETA * D_FF ** -0.5),
        'b_down': nrm((DEPTH, N_EXPERTS, d), 0.01),
    }


def reference(x_prompt, x_sample, cache_k, cache_v, cache_logf, state_rwkv, state_shift, state_hgrn, page_table,
              w_in, rw_mu, rw_w0, rw_w_up, rw_a0, rw_a_up, rw_g_up, rw_k_k, rw_k_a, rw_r_k, rw_gn_g, rw_gn_b,
              hg_gamma, hg_norm_g, fx_bf, p_rw, p_hg, p_fx, w_o, ln1_g, ln1_b, ln2_g, ln2_b,
              w_router, b_router, w_up, b_up, w_down, b_down):
    n_dec = x_sample.shape[0]
    past_len = page_table.shape[1] * cache_k.shape[2]
    p_soft = jax.nn.softmax(hg_gamma.astype(F32), axis=0)
    hg_lb = jnp.cumsum(p_soft, axis=0) - p_soft[0:1]
    xp, xs = x_prompt, x_sample
    bp = xp.shape[0]
    st_p, st_s = [], []
    for l in range(DEPTH):
        lp = {'w_in': w_in[l], 'rw_mu': rw_mu[l], 'rw_w0': rw_w0[l], 'rw_w_up': rw_w_up[l], 'rw_a0': rw_a0[l],
              'rw_a_up': rw_a_up[l], 'rw_g_up': rw_g_up[l], 'rw_k_k': rw_k_k[l], 'rw_k_a': rw_k_a[l],
              'rw_r_k': rw_r_k[l], 'rw_gn_g': rw_gn_g[l], 'rw_gn_b': rw_gn_b[l], 'hg_lb': hg_lb[l],
              'hg_norm_g': hg_norm_g[l], 'fx_bf': fx_bf[l], 'p_rw': p_rw[l], 'p_hg': p_hg[l], 'p_fx': p_fx[l],
              'w_o': w_o[l], 'w_router': w_router[l], 'b_router': b_router[l], 'w_up': w_up[l],
              'b_up': b_up[l], 'w_down': w_down[l], 'b_down': b_down[l]}
        mix, sp = token_mixer(xp, lp, jnp.zeros((bp, RW_COLS), xp.dtype),
                              jnp.zeros((bp, RW_HEADS, RW_HD, RW_HD), F32),
                              jnp.zeros((bp, HG_HEADS, HG_DK, HG_DV), F32), None)
        xp = layer_norm(DN_ALPHA * xp + mix, ln1_g[l], ln1_b[l])
        xp = layer_norm(DN_ALPHA * xp + moe_ffn(xp, lp), ln2_g[l], ln2_b[l])
        past = (cache_k[l][page_table].reshape(n_dec, past_len, FX_HEADS, FX_HD),
                cache_v[l][page_table].reshape(n_dec, past_len, FX_HEADS, FX_HD),
                cache_logf[l][page_table].reshape(n_dec, past_len, FX_HEADS).astype(F32))
        mix, ss = token_mixer(xs, lp, state_shift[l], state_rwkv[l], state_hgrn[l], past)
        xs = layer_norm(DN_ALPHA * xs + mix, ln1_g[l], ln1_b[l])
        xs = layer_norm(DN_ALPHA * xs + moe_ffn(xs, lp), ln2_g[l], ln2_b[l])
        st_p.append(sp)
        st_s.append(ss)
    stk = lambda sts, i: jnp.stack([s[i] for s in sts])
    return (xp, xs,
            stk(st_p, 0), stk(st_p, 1), stk(st_p, 2), stk(st_p, 3), stk(st_p, 4), stk(st_p, 5),
            stk(st_s, 0), stk(st_s, 1), stk(st_s, 2), stk(st_s, 3), stk(st_s, 4), stk(st_s, 5))
```

```python
import functools
import math

import numpy as np
import jax
import jax.numpy as jnp
from jax import lax
from jax.experimental import pallas as pl
from jax.experimental.pallas import tpu as pltpu

F32 = jnp.float32
BF16 = jnp.bfloat16
HIGHEST = lax.Precision.HIGHEST

D_MODEL = 2048
DEPTH = 4
PAGE_SIZE = 128
RW_WIDTH = D_MODEL // 2
RW_HD = 64
RW_HEADS = RW_WIDTH // RW_HD
RW_LORA_W = 64
RW_LORA_A = 64
RW_LORA_G = 160
RW_LORA = RW_LORA_W + RW_LORA_A + RW_LORA_G
RW_COLS = 3 * RW_WIDTH + RW_LORA
RW_GN_EPS = 1e-5 * RW_HD
HG_WIDTH = D_MODEL // 2
HG_DK = 128
HG_DV = 128
HG_HEADS = HG_WIDTH // HG_DK
HG_COLS = 4 * HG_WIDTH
HG_EPS = 1e-5
FX_WIDTH = D_MODEL // 2
FX_HD = 128
FX_HEADS = FX_WIDTH // FX_HD
FX_COLS = 3 * FX_WIDTH + FX_HEADS
FX_SCALE = FX_HD ** -0.5
N_BRANCH = 3
GATE_COLS = N_BRANCH * D_MODEL
N_EXPERTS = 32
TOP_K = 4
D_FF = D_MODEL
SWIGLU_LIMIT = 7.0
SWIGLU_ALPHA = 1.702
DN_ALPHA = (2 * DEPTH) ** 0.25
LN_EPS = 1e-5

LANES = 128
SUBLANES = 8
VMEM_LIMIT = 52 * 1024 * 1024

LORA_PAD = 512
LORA_USED = 384
CHUNK = 64
MOE_BM = 256


def _cparams(sem):
    return pltpu.CompilerParams(dimension_semantics=sem, vmem_limit_bytes=VMEM_LIMIT)


def _dot(a, b):
    return jnp.dot(a, b, preferred_element_type=F32)


def _dot_nt(a, b):
    return lax.dot_general(a, b, (((1,), (1,)), ((), ())), preferred_element_type=F32)


def _dot_tn(a, b):
    return lax.dot_general(a, b, (((0,), (0,)), ((), ())), preferred_element_type=F32)


def _hilo(x):
    hi = x.astype(BF16)
    lo = (x - hi.astype(F32)).astype(BF16)
    return hi, lo


def _dot_exact_lhs(m01, x):
    hi, lo = _hilo(x)
    return _dot(m01, hi) + _dot(m01, lo)


def _dot_exact_rhs(x, m01):
    hi, lo = _hilo(x)
    return _dot(hi, m01) + _dot(lo, m01)


def _log_sigmoid(z):
    return jnp.minimum(z, 0.0) - jnp.log1p(jnp.exp(-jnp.abs(z)))


def _mm_kernel(x_ref, w_ref, o_ref, *, act):
    acc = _dot(x_ref[...], w_ref[...])
    if act == "sigmoid":
        acc = jax.nn.sigmoid(acc)
    o_ref[...] = acc.astype(o_ref.dtype)


def _matmul(x, w, out_dtype, act=None, tm=1024, tn=512):
    m, k = x.shape
    n = w.shape[1]
    tm = min(tm, m)
    tn = min(tn, n)
    assert m % tm == 0 and n % tn == 0
    return pl.pallas_call(
        functools.partial(_mm_kernel, act=act),
        out_shape=jax.ShapeDtypeStruct((m, n), out_dtype),
        grid=(n // tn, m // tm),
        in_specs=[pl.BlockSpec((tm, k), lambda j, i: (i, 0)),
                  pl.BlockSpec((k, tn), lambda j, i: (0, j))],
        out_specs=pl.BlockSpec((tm, tn), lambda j, i: (i, j)),
        compiler_params=_cparams(("parallel", "parallel")),
        name="dense_matmul",
    )(x, w)


def _rwkv_pre_kernel(prkv_ref, plora_ref, sh_rkv_ref, sh_lora_ref, mu_rkv_ref, mu_lora_ref,
                     w0_ref, a0_ref, wup_ref, aup_ref, gup_ref, kk_ref, ka_ref, rk_ref, ones_ref,
                     r_o, lw_o, k_o, v_o, kk_o, a_o, bonus_o, g_o, carry_rkv, carry_lora):
    t = pl.program_id(1)

    @pl.when(t == 0)
    def _():
        carry_rkv[...] = sh_rkv_ref[...]
        carry_lora[...] = sh_lora_ref[...]

    def shifted(p, carry):
        rows = p.shape[0]
        prev = pltpu.roll(p, 1, 0)
        first = lax.broadcasted_iota(jnp.int32, p.shape, 0) == 0
        prev = jnp.where(first, carry[...], prev)
        carry[...] = p[rows - 1:rows, :]
        return prev

    p = prkv_ref[...]
    xs = p + mu_rkv_ref[...] * (shifted(p, carry_rkv) - p)
    pl_ = plora_ref[...]
    xl = pl_ + mu_lora_ref[...] * (shifted(pl_, carry_lora) - pl_)

    w = RW_WIDTH
    r = xs[:, 0:w]
    k0 = xs[:, w:2 * w]
    v = xs[:, 2 * w:3 * w]

    lane = lax.broadcasted_iota(jnp.int32, xl.shape, 1)
    act = jnp.where(lane < RW_LORA_W, jnp.tanh(xl),
                    jnp.where(lane < RW_LORA_W + RW_LORA_A, xl, jax.nn.sigmoid(xl))).astype(BF16)
    w_log = _log_sigmoid(w0_ref[...] + _dot(act, wup_ref[...])) - 0.5
    lw = -jnp.exp(w_log)
    a = jax.nn.sigmoid(a0_ref[...] + _dot(act, aup_ref[...]))
    g = _dot(act, gup_ref[...])

    ones = ones_ref[...]

    def segsum(x):
        return jnp.concatenate(
            [_dot_exact_rhs(x[:, j * LANES:(j + 1) * LANES], ones) for j in range(w // LANES)], axis=1)

    kkraw = k0 * kk_ref[...]
    kk = kkraw / jnp.maximum(jnp.sqrt(segsum(kkraw * kkraw)), 1e-12)
    k = k0 * (1.0 + (a - 1.0) * ka_ref[...])
    bonus = segsum(r * k * rk_ref[...]) * v

    r_o[...] = r
    lw_o[...] = lw
    k_o[...] = k
    v_o[...] = v
    kk_o[...] = kk
    a_o[...] = a
    bonus_o[...] = bonus
    g_o[...] = g


def _rwkv_pre(p_rkv, p_lora, sh_rkv, sh_lora, wl, batch, seq, tl):
    t_tot = batch * seq
    nt = seq // tl
    w = RW_WIDTH
    row = lambda b, t: (b * nt + t, 0)
    const = lambda b, t: (0, 0)
    out = jax.ShapeDtypeStruct((t_tot, w), F32)
    return pl.pallas_call(
        _rwkv_pre_kernel,
        out_shape=[out] * 8,
        grid=(batch, nt),
        in_specs=[pl.BlockSpec((tl, 3 * w), row),
                  pl.BlockSpec((tl, LORA_USED), row),
                  pl.BlockSpec((None, 1, 3 * w), lambda b, t: (b, 0, 0)),
                  pl.BlockSpec((None, 1, LORA_USED), lambda b, t: (b, 0, 0)),
                  pl.BlockSpec((1, 3 * w), const),
                  pl.BlockSpec((1, LORA_USED), const),
                  pl.BlockSpec((1, w), const),
                  pl.BlockSpec((1, w), const),
                  pl.BlockSpec((LORA_USED, w), const),
                  pl.BlockSpec((LORA_USED, w), const),
                  pl.BlockSpec((LORA_USED, w), const),
                  pl.BlockSpec((1, w), const),
                  pl.BlockSpec((1, w), const),
                  pl.BlockSpec((1, w), const),
                  pl.BlockSpec((LANES, LANES), const)],
        out_specs=[pl.BlockSpec((tl, w), row)] * 8,
        scratch_shapes=[pltpu.VMEM((1, 3 * w), F32), pltpu.VMEM((1, LORA_USED), F32)],
        compiler_params=_cparams(("parallel", "arbitrary")),
        name="rwkv_pre",
    )(p_rkv, p_lora, sh_rkv, sh_lora, wl["mu_rkv"], wl["mu_lora"], wl["w0"], wl["a0"],
      wl["wup"], wl["aup"], wl["gup"], wl["k_k"], wl["k_a"], wl["r_k"], wl["ones_head"])


def _rwkv_chunk_kernel(r_ref, lw_ref, k_ref, v_ref, kk_ref, a_ref, bonus_ref, g_ref, gng_ref, gnb_ref,
                       s0_ref, tri_ref, ones_ref, o_ref, sfin_ref, g_scr, *, chunk):
    c = pl.program_id(2)

    @pl.when(c == 0)
    def _():
        g_scr[...] = s0_ref[...]

    C = chunk
    r = r_ref[...]
    lw = lw_ref[...]
    k = k_ref[...]
    v = v_ref[...]
    kk = kk_ref[...]
    a = a_ref[...]

    cum = _dot_exact_lhs(tri_ref[...], lw)
    p_inc = jnp.exp(cum)
    p_inv = jnp.exp(-cum)
    p_prev = jnp.exp(cum - lw)
    kap = kk * p_prev
    bet = kk * a * p_inv
    kt = k * p_inv
    rt = r * p_inc

    head0 = lax.broadcasted_iota(jnp.int32, (C, LANES), 1) < RW_HD

    def expand(x):
        return jnp.concatenate([jnp.where(head0, x, 0.0), jnp.where(head0, 0.0, x)], axis=0)

    lm = jnp.concatenate([expand(kap), expand(rt)], axis=0).astype(BF16)
    rg = jnp.concatenate([expand(bet), expand(kt)], axis=0).astype(BF16)
    sc = _dot_nt(lm, rg)

    ri = lax.broadcasted_iota(jnp.int32, (2 * C, 2 * C), 0)
    ci = lax.broadcasted_iota(jnp.int32, (2 * C, 2 * C), 1)
    rin = jnp.bitwise_and(ri, C - 1)
    cin = jnp.bitwise_and(ci, C - 1)
    strict = rin > cin
    incl = rin >= cin
    a_b = jnp.where(strict, sc[0:2 * C, 0:2 * C], 0.0)
    a_k = jnp.where(strict, sc[0:2 * C, 2 * C:4 * C], 0.0)
    r_b = jnp.where(incl, sc[2 * C:4 * C, 0:2 * C], 0.0)
    r_k = jnp.where(incl, sc[2 * C:4 * C, 2 * C:4 * C], 0.0)

    n_pow = -a_b
    t_inv = jnp.where(ri == ci, 1.0, 0.0) + n_pow
    for _ in range(int(math.log2(C)) - 1):
        n_pow = jnp.dot(n_pow, n_pow, precision=HIGHEST, preferred_element_type=F32)
        t_inv = t_inv + jnp.dot(t_inv, n_pow, precision=HIGHEST, preferred_element_type=F32)

    g_state = g_scr[...]
    w0 = _dot_nt(lm, g_state.astype(BF16))
    ve = expand(v)
    z = w0[0:2 * C] + _dot(a_k.astype(BF16), ve.astype(BF16))
    u = -jnp.dot(t_inv, z, precision=HIGHEST, preferred_element_type=F32)
    uv = jnp.concatenate([u, ve], axis=0).astype(BF16)
    rbk = jnp.concatenate([r_b, r_k], axis=1).astype(BF16)
    ye = w0[2 * C:4 * C] + _dot(rbk, uv)
    y = ye[0:C] + ye[C:2 * C]

    p_last = p_inc[C - 1:C, :]
    g_new = (g_state + _dot_tn(uv, rg)) * p_last
    g_scr[...] = g_new
    sfin_ref[...] = g_new

    ones = ones_ref[...]
    mu = _dot_exact_rhs(y, ones) * (1.0 / RW_HD)
    d = y - mu
    var = _dot_exact_rhs(d * d, ones) * (1.0 / RW_HD)
    yn = d * lax.rsqrt(var + RW_GN_EPS) * gng_ref[...] + gnb_ref[...]
    o_ref[...] = ((yn + bonus_ref[...]) * g_ref[...]).astype(o_ref.dtype)


def _rwkv_chunk(pre, s0_pairs, wl, batch, seq):
    r, lw, k, v, kk, a, bonus, g = pre
    nc = seq // CHUNK
    npair = RW_WIDTH // LANES
    blk = pl.BlockSpec((CHUNK, LANES), lambda b, j, c: (b * nc + c, j))
    vec = pl.BlockSpec((1, LANES), lambda b, j, c: (0, j))
    st = pl.BlockSpec((None, None, LANES, LANES), lambda b, j, c: (b, j, 0, 0))
    return pl.pallas_call(
        functools.partial(_rwkv_chunk_kernel, chunk=CHUNK),
        out_shape=[jax.ShapeDtypeStruct((batch * seq, RW_WIDTH), BF16),
                   jax.ShapeDtypeStruct((batch, npair, LANES, LANES), F32)],
        grid=(batch, npair, nc),
        in_specs=[blk] * 8 + [vec, vec, st,
                              pl.BlockSpec((CHUNK, CHUNK), lambda b, j, c: (0, 0)),
                              pl.BlockSpec((LANES, LANES), lambda b, j, c: (0, 0))],
        out_specs=[blk, st],
        scratch_shapes=[pltpu.VMEM((LANES, LANES), F32)],
        compiler_params=_cparams(("parallel", "parallel", "arbitrary")),
        name="rwkv_chunk",
    )(r, lw, k, v, kk, a, bonus, g, wl["gn_g"], wl["gn_b"], s0_pairs, wl["tri"], wl["ones_head"])


def _hgrn_masks(chunk):
    nl = int(math.log2(chunk))
    t = np.arange(chunk)[:, None]
    j = np.arange(chunk)[None, :]
    mats = [(j <= t)]
    for lvl in range(nl):
        m = 1 << lvl
        mid = (t // (2 * m)) * 2 * m + m
        mats.append(((t & m) != 0) & (j >= mid) & (j <= t))
    for lvl in range(nl):
        m = 1 << lvl
        mid = (t // (2 * m)) * 2 * m + m
        mats.append(((t & m) == 0) & (j >= t + 1) & (j <= mid - 1))
    return np.concatenate(mats, axis=0).astype(np.float32)


def _hgrn_kernel(hq_ref, hf_ref, hi_ref, hz_ref, loglb_ref, log1mlb_ref, omlb_ref, ng_ref, s0_ref, mall_ref,
                 o_ref, sfin_ref, g_scr, *, chunk, n_valid):
    c = pl.program_id(2)

    @pl.when(c == 0)
    def _():
        g_scr[...] = s0_ref[...]

    C = chunk
    nl = int(math.log2(C))
    z = hf_ref[...]
    la = loglb_ref[...]
    lb_ = log1mlb_ref[...] + _log_sigmoid(z)
    lf = jnp.maximum(la, lb_) + jnp.log1p(jnp.exp(-jnp.abs(la - lb_)))
    kx = omlb_ref[...] * jax.nn.sigmoid(-z)
    hq = hq_ref[...]
    q = hq * jax.nn.sigmoid(hq)
    iv = hi_ref[...]
    rowi = lax.broadcasted_iota(jnp.int32, (C, LANES), 0)
    if n_valid < C:
        valid = rowi < n_valid
        lf = jnp.where(valid, lf, 0.0)
        kx = jnp.where(valid, kx, 0.0)

    dall = _dot_exact_lhs(mall_ref[...], lf)
    b = dall[0:C]
    ri = lax.broadcasted_iota(jnp.int32, (C, C), 0)
    ci = lax.broadcasted_iota(jnp.int32, (C, C), 1)
    amat = jnp.where(ri == ci, jnp.sum(q * kx, axis=1, keepdims=True), 0.0)
    for lvl in range(nl):
        m = 1 << lvl
        d = dall[(1 + lvl) * C:(2 + lvl) * C]
        e = dall[(1 + nl + lvl) * C:(2 + nl + lvl) * C]
        second = jnp.bitwise_and(rowi, m) != 0
        qt = jnp.where(second, q * jnp.exp(d), 0.0).astype(BF16)
        kl = jnp.where(second, 0.0, kx * jnp.exp(e)).astype(BF16)
        same = jnp.right_shift(ri, lvl + 1) == jnp.right_shift(ci, lvl + 1)
        amat = amat + jnp.where(same, _dot_nt(qt, kl), 0.0)

    g_state = g_scr[...]
    ivb = iv.astype(BF16)
    o = _dot_nt((q * jnp.exp(b)).astype(BF16), g_state.astype(BF16)) + _dot(amat.astype(BF16), ivb)
    b_last = b[C - 1:C, :]
    khat = (kx * jnp.exp(b_last - b)).astype(BF16)
    g_new = g_state * jnp.exp(b_last) + _dot_tn(ivb, khat)
    g_scr[...] = g_new
    sfin_ref[...] = g_new

    ms = jnp.mean(o * o, axis=1, keepdims=True)
    hz = hz_ref[...]
    o_ref[...] = (o * lax.rsqrt(ms + HG_EPS) * ng_ref[...] * (hz * jax.nn.sigmoid(hz))).astype(o_ref.dtype)


def _hgrn(p_hg, s0_t, wl, batch, seq_pad, n_valid):
    nc = seq_pad // CHUNK
    nh = HG_HEADS

    def blk(part):
        return pl.BlockSpec((CHUNK, LANES), lambda b, h, c: (b * nc + c, part * nh + h))

    vec = pl.BlockSpec((1, LANES), lambda b, h, c: (0, h))
    st = pl.BlockSpec((None, None, LANES, LANES), lambda b, h, c: (b, h, 0, 0))
    mall = wl["hg_mall"]
    return pl.pallas_call(
        functools.partial(_hgrn_kernel, chunk=CHUNK, n_valid=n_valid),
        out_shape=[jax.ShapeDtypeStruct((batch * seq_pad, HG_WIDTH), BF16),
                   jax.ShapeDtypeStruct((batch, nh, LANES, LANES), F32)],
        grid=(batch, nh, nc),
        in_specs=[blk(0), blk(1), blk(2), blk(3), vec, vec, vec,
                  pl.BlockSpec((1, LANES), lambda b, h, c: (0, 0)), st,
                  pl.BlockSpec(mall.shape, lambda b, h, c: (0, 0))],
        out_specs=[pl.BlockSpec((CHUNK, LANES), lambda b, h, c: (b * nc + c, h)), st],
        scratch_shapes=[pltpu.VMEM((LANES, LANES), F32)],
        compiler_params=_cparams(("parallel", "parallel", "arbitrary")),
        name="hgrn_chunk",
    )(p_hg, p_hg, p_hg, p_hg, wl["hg_loglb"], wl["hg_log1mlb"], wl["hg_omlb"], wl["hg_ng"], s0_t, mall)


def _fox_prompt_kernel(q_ref, k_ref, v_ref, negc_ref, o_ref, m_scr, l_scr, acc_scr, *, tq, tk):
    qi = pl.program_id(2)
    ki = pl.program_id(3)

    @pl.when(ki == 0)
    def _():
        m_scr[...] = jnp.full(m_scr.shape, -jnp.inf, F32)
        l_scr[...] = jnp.zeros(l_scr.shape, F32)
        acc_scr[...] = jnp.zeros(acc_scr.shape, F32)

    @pl.when(ki * tk <= qi * tq + tq - 1)
    def _():
        s = _dot_nt(q_ref[...].astype(BF16), k_ref[...].astype(BF16)) * FX_SCALE + negc_ref[...]
        rows = qi * tq + lax.broadcasted_iota(jnp.int32, (tq, tk), 0)
        cols = ki * tk + lax.broadcasted_iota(jnp.int32, (tq, tk), 1)
        s = jnp.where(cols <= rows, s, -jnp.inf)
        m_prev = m_scr[...]
        m_new = jnp.maximum(m_prev, jnp.max(s, axis=1, keepdims=True))
        p = jnp.exp(s - m_new)
        alpha = jnp.exp(m_prev - m_new)
        l_scr[...] = alpha * l_scr[...] + jnp.sum(p, axis=1, keepdims=True)
        acc_scr[...] = alpha * acc_scr[...] + _dot(p.astype(BF16), v_ref[...].astype(BF16))
        m_scr[...] = m_new

    @pl.when(ki == pl.num_programs(3) - 1)
    def _():
        o_ref[...] = (acc_scr[...] / l_scr[...]).astype(o_ref.dtype)


def _fox_prompt(p_fx, negc, batch, seq, tq=512, tk=512):
    nq = seq // tq
    nk = seq // tk
    nh = FX_HEADS

    def kv_row(b, qi, ki):
        return b * nk + jnp.minimum(ki, (qi * tq + tq - 1) // tk)

    return pl.pallas_call(
        functools.partial(_fox_prompt_kernel, tq=tq, tk=tk),
        out_shape=jax.ShapeDtypeStruct((batch * seq, FX_WIDTH), BF16),
        grid=(batch, nh, nq, nk),
        in_specs=[pl.BlockSpec((tq, LANES), lambda b, h, qi, ki: (b * nq + qi, h)),
                  pl.BlockSpec((tk, LANES), lambda b, h, qi, ki: (kv_row(b, qi, ki), nh + h)),
                  pl.BlockSpec((tk, LANES), lambda b, h, qi, ki: (kv_row(b, qi, ki), 2 * nh + h)),
                  pl.BlockSpec((None, 1, tk),
                               lambda b, h, qi, ki: (b * nh + h, 0, jnp.minimum(ki, (qi * tq + tq - 1) // tk)))],
        out_specs=pl.BlockSpec((tq, LANES), lambda b, h, qi, ki: (b * nq + qi, h)),
        scratch_shapes=[pltpu.VMEM((tq, 1), F32), pltpu.VMEM((tq, 1), F32), pltpu.VMEM((tq, LANES), F32)],
        compiler_params=_cparams(("parallel", "parallel", "parallel", "arbitrary")),
        name="fox_prompt",
    )(p_fx, p_fx, p_fx, negc)


def _fox_sample_kernel(pt_ref, q_ref, kp_ref, vp_ref, negcp_ref, negcn_ref, o_ref, m_scr, l_scr, acc_scr,
                       *, n_new):
    j = pl.program_id(1)
    nh = FX_HEADS

    @pl.when(j == 0)
    def _():
        m_scr[...] = jnp.full(m_scr.shape, -jnp.inf, F32)
        l_scr[...] = jnp.zeros(l_scr.shape, F32)
        acc_scr[...] = jnp.zeros(acc_scr.shape, F32)

    def update(h, s, vals):
        m_prev = m_scr[h]
        m_new = jnp.maximum(m_prev, jnp.max(s, axis=1, keepdims=True))
        p = jnp.exp(s - m_new)
        alpha = jnp.exp(m_prev - m_new)
        l_scr[h] = alpha * l_scr[h] + jnp.sum(p, axis=1, keepdims=True)
        acc_scr[h] = alpha * acc_scr[h] + _dot(p.astype(BF16), vals)
        m_scr[h] = m_new

    for h in range(nh):
        qh = q_ref[:, h * LANES:(h + 1) * LANES].astype(BF16)
        kh = kp_ref[pl.ds(h, PAGE_SIZE, stride=nh), :].astype(BF16)
        vh = vp_ref[pl.ds(h, PAGE_SIZE, stride=nh), :].astype(BF16)
        s = _dot_nt(qh, kh) * FX_SCALE + negcp_ref[h:h + 1, :]
        update(h, s, vh)

    @pl.when(j == pl.num_programs(1) - 1)
    def _():
        rows = lax.broadcasted_iota(jnp.int32, (n_new, n_new), 0)
        cols = lax.broadcasted_iota(jnp.int32, (n_new, n_new), 1)
        for h in range(nh):
            qh = q_ref[:, h * LANES:(h + 1) * LANES].astype(BF16)
            kh = q_ref[:, (nh + h) * LANES:(nh + h + 1) * LANES].astype(BF16)
            vh = q_ref[:, (2 * nh + h) * LANES:(2 * nh + h + 1) * LANES].astype(BF16)
            s = _dot_nt(qh, kh) * FX_SCALE + negcn_ref[h:h + 1, :]
            s = jnp.where(cols <= rows, s, -jnp.inf)
            update(h, s, vh)
            o_ref[:, h * LANES:(h + 1) * LANES] = (acc_scr[h] / l_scr[h]).astype(o_ref.dtype)


def _fox_sample(page_table, p_fx, cache_k, cache_v, negc_past, negc_new, layer, batch, n_new):
    n_pages = page_table.shape[1]
    nh = FX_HEADS
    rows = PAGE_SIZE * nh
    grid_spec = pltpu.PrefetchScalarGridSpec(
        num_scalar_prefetch=1,
        grid=(batch, n_pages),
        in_specs=[pl.BlockSpec((n_new, 3 * FX_WIDTH), lambda b, j, pt: (b, 0)),
                  pl.BlockSpec((None, None, rows, LANES), lambda b, j, pt: (layer, pt[b, j], 0, 0)),
                  pl.BlockSpec((None, None, rows, LANES), lambda b, j, pt: (layer, pt[b, j], 0, 0)),
                  pl.BlockSpec((None, nh, PAGE_SIZE), lambda b, j, pt: (b, 0, j)),
                  pl.BlockSpec((None, nh, n_new), lambda b, j, pt: (b, 0, 0))],
        out_specs=pl.BlockSpec((n_new, FX_WIDTH), lambda b, j, pt: (b, 0)),
        scratch_shapes=[pltpu.VMEM((nh, n_new, 1), F32), pltpu.VMEM((nh, n_new, 1), F32),
                        pltpu.VMEM((nh, n_new, LANES), F32)])
    return pl.pallas_call(
        functools.partial(_fox_sample_kernel, n_new=n_new),
        out_shape=jax.ShapeDtypeStruct((batch * n_new, FX_WIDTH), BF16),
        grid_spec=grid_spec,
        compiler_params=_cparams(("parallel", "arbitrary")),
        name="fox_sample",
    )(page_table, p_fx, cache_k, cache_v, negc_past, negc_new)


def _merge_kernel(orw_ref, ohg_ref, ofx_ref, g0_ref, g1_ref, g2_ref, prw_ref, phg_ref, pfx_ref, o_ref):
    acc = g0_ref[...].astype(F32) * _dot(orw_ref[...], prw_ref[...])
    acc = acc + g1_ref[...].astype(F32) * _dot(ohg_ref[...], phg_ref[...])
    acc = acc + g2_ref[...].astype(F32) * _dot(ofx_ref[...], pfx_ref[...])
    o_ref[...] = acc.astype(o_ref.dtype)


def _merge(o_rw, o_hg, o_fx, gates, wl, tm=512, tn=1024):
    m = o_rw.shape[0]
    tm = min(tm, m)
    assert m % tm == 0
    nn = D_MODEL // tn
    w = RW_WIDTH
    ob = pl.BlockSpec((tm, w), lambda j, i: (i, 0))

    def gb(br):
        return pl.BlockSpec((tm, tn), lambda j, i: (i, br * nn + j))

    wb = pl.BlockSpec((w, tn), lambda j, i: (0, j))
    return pl.pallas_call(
        _merge_kernel,
        out_shape=jax.ShapeDtypeStruct((m, D_MODEL), BF16),
        grid=(nn, m // tm),
        in_specs=[ob, ob, ob, gb(0), gb(1), gb(2), wb, wb, wb],
        out_specs=pl.BlockSpec((tm, tn), lambda j, i: (i, j)),
        compiler_params=_cparams(("parallel", "parallel")),
        name="branch_merge",
    )(o_rw, o_hg, o_fx, gates, gates, gates, wl["p_rw"], wl["p_hg"], wl["p_fx"])


def _layer_norm_rows(x, g, b):
    mu = jnp.mean(x, axis=1, keepdims=True)
    d = x - mu
    var = jnp.mean(d * d, axis=1, keepdims=True)
    return d * lax.rsqrt(var + LN_EPS) * g + b


def _mm_res_ln_kernel(a_ref, w_ref, x_ref, g_ref, b_ref, of_ref, ob_ref):
    y = DN_ALPHA * x_ref[...] + _dot(a_ref[...], w_ref[...])
    out = _layer_norm_rows(y, g_ref[...], b_ref[...])
    of_ref[...] = out
    ob_ref[...] = out.astype(BF16)


def _mm_res_ln(a, w, x, g, b, tm=256):
    m, k = a.shape
    tm = min(tm, m)
    assert m % tm == 0
    d = D_MODEL
    row = lambda i: (i, 0)
    const = lambda i: (0, 0)
    return pl.pallas_call(
        _mm_res_ln_kernel,
        out_shape=[jax.ShapeDtypeStruct((m, d), F32), jax.ShapeDtypeStruct((m, d), BF16)],
        grid=(m // tm,),
        in_specs=[pl.BlockSpec((tm, k), row), pl.BlockSpec((k, d), const), pl.BlockSpec((tm, d), row),
                  pl.BlockSpec((1, d), const), pl.BlockSpec((1, d), const)],
        out_specs=[pl.BlockSpec((tm, d), row), pl.BlockSpec((tm, d), row)],
        compiler_params=_cparams(("parallel",)),
        name="outproj_res_ln",
    )(a, w, x, g, b)


def _res_ln_kernel(y_ref, x_ref, g_ref, b_ref, of_ref, ob_ref):
    out = _layer_norm_rows(DN_ALPHA * x_ref[...] + y_ref[...], g_ref[...], b_ref[...])
    of_ref[...] = out
    ob_ref[...] = out.astype(BF16)


def _res_ln(y, x, g, b, tm=256):
    m, d = x.shape
    tm = min(tm, m)
    assert m % tm == 0
    row = lambda i: (i, 0)
    const = lambda i: (0, 0)
    return pl.pallas_call(
        _res_ln_kernel,
        out_shape=[jax.ShapeDtypeStruct((m, d), F32), jax.ShapeDtypeStruct((m, d), BF16)],
        grid=(m // tm,),
        in_specs=[pl.BlockSpec((tm, d), row), pl.BlockSpec((tm, d), row),
                  pl.BlockSpec((1, d), const), pl.BlockSpec((1, d), const)],
        out_specs=[pl.BlockSpec((tm, d), row), pl.BlockSpec((tm, d), row)],
        compiler_params=_cparams(("parallel",)),
        name="res_ln",
    )(y, x, g, b)


def _router_kernel(x_ref, w_ref, b_ref, o_ref):
    o_ref[...] = jnp.dot(x_ref[...], w_ref[...], precision=HIGHEST, preferred_element_type=F32) + b_ref[...]


def _router(x, w_pad, b_pad, tm=512):
    m, d = x.shape
    tm = min(tm, m)
    assert m % tm == 0
    return pl.pallas_call(
        _router_kernel,
        out_shape=jax.ShapeDtypeStruct((m, LANES), F32),
        grid=(m // tm,),
        in_specs=[pl.BlockSpec((tm, d), lambda i: (i, 0)), pl.BlockSpec((d, LANES), lambda i: (0, 0)),
                  pl.BlockSpec((1, LANES), lambda i: (0, 0))],
        out_specs=pl.BlockSpec((tm, LANES), lambda i: (i, 0)),
        compiler_params=_cparams(("parallel",)),
        name="moe_router",
    )(x, w_pad, b_pad)


def _moe_up_kernel(be_ref, nu_ref, x_ref, wg_ref, wl_ref, bg_ref, bl_ref, o_ref):
    blk = pl.program_id(1)

    @pl.when(blk < nu_ref[0])
    def _():
        x = x_ref[...]
        glu = jnp.minimum(_dot(x, wg_ref[...]) + bg_ref[...], SWIGLU_LIMIT)
        lin = jnp.clip(_dot(x, wl_ref[...]) + bl_ref[...], -SWIGLU_LIMIT, SWIGLU_LIMIT)
        o_ref[...] = (glu * jax.nn.sigmoid(SWIGLU_ALPHA * glu) * (lin + 1.0)).astype(o_ref.dtype)

    @pl.when(blk >= nu_ref[0])
    def _():
        o_ref[...] = jnp.zeros(o_ref.shape, o_ref.dtype)


def _moe_up(block_e, n_used, x_sorted, w_glu, w_lin, b_glu, b_lin, tn=512):
    n_rows, d = x_sorted.shape
    n_blocks = n_rows // MOE_BM
    grid_spec = pltpu.PrefetchScalarGridSpec(
        num_scalar_prefetch=2,
        grid=(D_FF // tn, n_blocks),
        in_specs=[pl.BlockSpec((MOE_BM, d), lambda n, i, be, nu: (i, 0)),
                  pl.BlockSpec((None, d, tn), lambda n, i, be, nu: (be[i], 0, n)),
                  pl.BlockSpec((None, d, tn), lambda n, i, be, nu: (be[i], 0, n)),
                  pl.BlockSpec((None, 1, tn), lambda n, i, be, nu: (be[i], 0, n)),
                  pl.BlockSpec((None, 1, tn), lambda n, i, be, nu: (be[i], 0, n))],
        out_specs=pl.BlockSpec((MOE_BM, tn), lambda n, i, be, nu: (i, n)))
    return pl.pallas_call(
        _moe_up_kernel,
        out_shape=jax.ShapeDtypeStruct((n_rows, D_FF), BF16),
        grid_spec=grid_spec,
        compiler_params=_cparams(("parallel", "arbitrary")),
        name="moe_up",
    )(block_e, n_used, x_sorted, w_glu, w_lin, b_glu, b_lin)


def _moe_down_kernel(be_ref, nu_ref, a_ref, w_ref, b_ref, gate_ref, o_ref):
    blk = pl.program_id(1)

    @pl.when(blk < nu_ref[0])
    def _():
        o_ref[...] = (_dot(a_ref[...], w_ref[...]) + b_ref[...]) * gate_ref[...]

    @pl.when(blk >= nu_ref[0])
    def _():
        o_ref[...] = jnp.zeros(o_ref.shape, o_ref.dtype)


def _moe_down(block_e, n_used, act, w_down, b_down, row_gate, tn=512):
    n_rows, f = act.shape
    n_blocks = n_rows // MOE_BM
    grid_spec = pltpu.PrefetchScalarGridSpec(
        num_scalar_prefetch=2,
        grid=(D_MODEL // tn, n_blocks),
        in_specs=[pl.BlockSpec((MOE_BM, f), lambda n, i, be, nu: (i, 0)),
                  pl.BlockSpec((None, f, tn), lambda n, i, be, nu: (be[i], 0, n)),
                  pl.BlockSpec((None, 1, tn), lambda n, i, be, nu: (be[i], 0, n)),
                  pl.BlockSpec((MOE_BM, 1), lambda n, i, be, nu: (i, 0))],
        out_specs=pl.BlockSpec((MOE_BM, tn), lambda n, i, be, nu: (i, n)))
    return pl.pallas_call(
        _moe_down_kernel,
        out_shape=jax.ShapeDtypeStruct((n_rows, D_MODEL), F32),
        grid_spec=grid_spec,
        compiler_params=_cparams(("parallel", "arbitrary")),
        name="moe_down",
    )(block_e, n_used, act, w_down, b_down, row_gate)


def _moe(xs_f32, x_bf16, wl):
    t = x_bf16.shape[0]
    tk = t * TOP_K
    logits = jnp.concatenate([_router(x, wl["w_router"], wl["b_router"]) for x in xs_f32], axis=0)[:, :N_EXPERTS]
    top_vals, top_idx = lax.top_k(logits, TOP_K)
    gates = jax.nn.softmax(top_vals, axis=-1)
    flat_e = top_idx.reshape(tk)
    flat_tok = jnp.arange(tk, dtype=jnp.int32) // TOP_K
    order = jnp.argsort(flat_e)
    e_sorted = flat_e[order]
    counts = jnp.bincount(flat_e, length=N_EXPERTS)
    starts = jnp.cumsum(counts) - counts
    padded = (counts + MOE_BM - 1) // MOE_BM * MOE_BM
    pends = jnp.cumsum(padded)
    pstarts = pends - padded
    dest = (pstarts[e_sorted] + (jnp.arange(tk, dtype=jnp.int32) - starts[e_sorted])).astype(jnp.int32)
    n_blocks = -(-(tk + N_EXPERTS * (MOE_BM - 1)) // MOE_BM)
    n_rows = n_blocks * MOE_BM
    row_tok = jnp.full((n_rows,), t, jnp.int32).at[dest].set(flat_tok[order])
    row_gate = jnp.zeros((n_rows,), F32).at[dest].set(gates.reshape(tk)[order])
    pos = jnp.zeros((tk,), jnp.int32).at[order].set(dest)
    block_e = jnp.minimum(jnp.searchsorted(pends, jnp.arange(n_blocks) * MOE_BM, side="right"),
                          N_EXPERTS - 1).astype(jnp.int32)
    n_used = (pends[-1:] // MOE_BM).astype(jnp.int32)
    x_pad = jnp.concatenate([x_bf16, jnp.zeros((1, D_MODEL), BF16)], axis=0)
    x_sorted = x_pad[row_tok]
    act = _moe_up(block_e, n_used, x_sorted, wl["w_glu"], wl["w_lin"], wl["b_glu"], wl["b_lin"])
    y_rows = _moe_down(block_e, n_used, act, wl["w_down"], wl["b_down"], row_gate[:, None])
    return y_rows[pos].reshape(t, TOP_K, D_MODEL).sum(axis=1)


def _layer_weights(l, w_in, rw_mu, rw_w0, rw_w_up, rw_a0, rw_a_up, rw_g_up, rw_k_k, rw_k_a, rw_r_k, rw_gn_g,
                   rw_gn_b, hg_lb, hg_norm_g, fx_bf, p_rw, p_hg, p_fx, w_o, ln1_g, ln1_b, ln2_g, ln2_b,
                   w_router, b_router, w_up, b_up, w_down, b_down, consts):
    w = RW_WIDTH
    wi = w_in[l]
    o_hg = RW_COLS
    o_fx = o_hg + HG_COLS
    o_gate = o_fx + FX_COLS
    lora_w = jnp.concatenate([wi[:, 3 * w:RW_COLS], wi[:, o_fx + 3 * FX_WIDTH:o_gate]], axis=1)
    lora_w = jnp.pad(lora_w, ((0, 0), (0, LORA_PAD - lora_w.shape[1])))

    def pad_rows(m, start):
        return jnp.pad(m, ((start, LORA_USED - start - m.shape[0]), (0, 0))).astype(BF16)

    mu = rw_mu[l]
    lb = hg_lb[l]
    wl = {
        "w_rkv": wi[:, 0:3 * w].astype(BF16),
        "w_lora": lora_w.astype(BF16),
        "w_hg": wi[:, o_hg:o_fx].astype(BF16),
        "w_fx": wi[:, o_fx:o_fx + 3 * FX_WIDTH].astype(BF16),
        "w_gate": wi[:, o_gate:].astype(BF16),
        "mu_rkv": mu[None, 0:3 * w],
        "mu_lora": jnp.pad(mu[3 * w:], (0, LORA_USED - RW_LORA))[None, :],
        "w0": rw_w0[l][None, :],
        "a0": rw_a0[l][None, :],
        "wup": pad_rows(rw_w_up[l], 0),
        "aup": pad_rows(rw_a_up[l], RW_LORA_W),
        "gup": pad_rows(rw_g_up[l], RW_LORA_W + RW_LORA_A),
        "k_k": rw_k_k[l][None, :],
        "k_a": rw_k_a[l][None, :],
        "r_k": rw_r_k[l].reshape(1, w),
        "gn_g": rw_gn_g[l][None, :],
        "gn_b": rw_gn_b[l][None, :],
        "ones_head": consts["ones_head"],
        "tri": consts["tri"],
        "hg_mall": consts["hg_mall"],
        "hg_loglb": jnp.log(lb)[None, :],
        "hg_log1mlb": jnp.log1p(-lb)[None, :],
        "hg_omlb": (1.0 - lb)[None, :],
        "hg_ng": hg_norm_g[l][None, :],
        "fx_bf": fx_bf[l],
        "p_rw": p_rw[l].astype(BF16),
        "p_hg": p_hg[l].astype(BF16),
        "p_fx": p_fx[l].astype(BF16),
        "w_o": w_o[l].astype(BF16),
        "ln1_g": ln1_g[l][None, :], "ln1_b": ln1_b[l][None, :],
        "ln2_g": ln2_g[l][None, :], "ln2_b": ln2_b[l][None, :],
        "w_router": jnp.pad(w_router[l], ((0, 0), (0, LANES - N_EXPERTS))),
        "b_router": jnp.pad(b_router[l], (0, LANES - N_EXPERTS))[None, :],
        "w_glu": w_up[l][:, :, 0::2].astype(BF16),
        "w_lin": w_up[l][:, :, 1::2].astype(BF16),
        "b_glu": b_up[l][:, None, 0::2],
        "b_lin": b_up[l][:, None, 1::2],
        "w_down": w_down[l].astype(BF16),
        "b_down": b_down[l][:, None, :],
    }
    return wl


def _token_mixer(x_bf16, wl, batch, seq, shift_prev, s_rw, s_hg, past, layer, page_table):
    t_tot = batch * seq
    w = RW_WIDTH
    p_rkv = _matmul(x_bf16, wl["w_rkv"], F32)
    p_lora = _matmul(x_bf16, wl["w_lora"], F32)
    p_hg = _matmul(x_bf16, wl["w_hg"], F32)
    p_fx = _matmul(x_bf16, wl["w_fx"], F32)
    gates = _matmul(x_bf16, wl["w_gate"], BF16, act="sigmoid")

    seq_pad = -(-seq // CHUNK) * CHUNK

    def pad_time(arr):
        if seq_pad == seq:
            return arr
        a3 = arr.reshape(batch, seq, arr.shape[-1])
        return jnp.pad(a3, ((0, 0), (0, seq_pad - seq), (0, 0))).reshape(batch * seq_pad, arr.shape[-1])

    def unpad_time(arr):
        if seq_pad == seq:
            return arr
        return arr.reshape(batch, seq_pad, arr.shape[-1])[:, :seq].reshape(t_tot, arr.shape[-1])

    sh_rkv = shift_prev[:, None, 0:3 * w]
    sh_lora = jnp.pad(shift_prev[:, 3 * w:], ((0, 0), (0, LORA_USED - RW_LORA)))[:, None, :]
    pre = _rwkv_pre(p_rkv, p_lora, sh_rkv, sh_lora, wl, batch, seq, min(seq, 256))
    pre = [pad_time(a) for a in pre]
    npair = w // LANES
    s_pairs = jnp.zeros((batch, npair, LANES, LANES), F32)
    s5 = s_rw.reshape(batch, npair, 2, RW_HD, RW_HD)
    s_pairs = s_pairs.at[:, :, :RW_HD, :RW_HD].set(s5[:, :, 0]).at[:, :, RW_HD:, RW_HD:].set(s5[:, :, 1])
    o_rw, s_fin = _rwkv_chunk(pre, s_pairs, wl, batch, seq_pad)
    o_rw = unpad_time(o_rw)
    new_srw = jnp.stack([s_fin[:, :, :RW_HD, :RW_HD], s_fin[:, :, RW_HD:, RW_HD:]], axis=2)
    new_srw = new_srw.reshape(batch, RW_HEADS, RW_HD, RW_HD)
    p3 = p_rkv.reshape(batch, seq, 3 * w)[:, -1]
    l3 = p_lora.reshape(batch, seq, LORA_PAD)[:, -1, :RW_LORA]
    new_shift = jnp.concatenate([p3, l3], axis=1)

    o_hg, g_fin = _hgrn(pad_time(p_hg), jnp.swapaxes(s_hg, 2, 3), wl, batch, seq_pad, min(seq, CHUNK))
    o_hg = unpad_time(o_hg)
    new_shg = jnp.swapaxes(g_fin, 2, 3)

    fz = p_lora[:, RW_LORA:RW_LORA + FX_HEADS].reshape(batch, seq, FX_HEADS)
    lf = jax.nn.log_sigmoid(fz + wl["fx_bf"])
    c_new = jnp.cumsum(lf, axis=1)
    negc_new = -jnp.swapaxes(c_new, 1, 2)
    if past is None:
        o_fx = _fox_prompt(p_fx, negc_new.reshape(batch * FX_HEADS, 1, seq), batch, seq)
    else:
        cache_k, cache_v, cache_logf = past
        lf_past = cache_logf[layer][page_table].reshape(batch, -1, FX_HEADS).astype(F32)
        c_past = -(lax.cumsum(lf_past, axis=1, reverse=True) - lf_past)
        negc_past = -jnp.swapaxes(c_past, 1, 2)
        o_fx = _fox_sample(page_table, p_fx, cache_k, cache_v, negc_past, negc_new, layer, batch, seq)
    fk = p_fx[:, FX_WIDTH:2 * FX_WIDTH].reshape(batch, seq, FX_HEADS, FX_HD)
    fv = p_fx[:, 2 * FX_WIDTH:].reshape(batch, seq, FX_HEADS, FX_HD)
    return (o_rw, o_hg, o_fx, gates), (fk, fv, lf, new_srw, new_shift, new_shg)


def kernel(x_prompt, x_sample, cache_k, cache_v, cache_logf, state_rwkv, state_shift, state_hgrn, page_table,
           w_in, rw_mu, rw_w0, rw_w_up, rw_a0, rw_a_up, rw_g_up, rw_k_k, rw_k_a, rw_r_k, rw_gn_g, rw_gn_b,
           hg_gamma, hg_norm_g, fx_bf, p_rw, p_hg, p_fx, w_o, ln1_g, ln1_b, ln2_g, ln2_b,
           w_router, b_router, w_up, b_up, w_down, b_down):
    bp, lp_, d = x_prompt.shape
    bs, ls_, _ = x_sample.shape
    tp = bp * lp_
    ts = bs * ls_
    n_pool = cache_k.shape[1]

    p_soft = jax.nn.softmax(hg_gamma.astype(F32), axis=0)
    hg_lb = jnp.maximum(jnp.cumsum(p_soft, axis=0) - p_soft[0:1], 0.0)

    head = np.arange(LANES) // RW_HD
    tri_np = (np.arange(CHUNK)[None, :] <= np.arange(CHUNK)[:, None]).astype(np.float32)
    consts = {
        "ones_head": jnp.asarray((head[:, None] == head[None, :]).astype(np.float32), BF16),
        "tri": jnp.asarray(tri_np, BF16),
        "hg_mall": jnp.asarray(_hgrn_masks(CHUNK), BF16),
    }
    ck = cache_k.reshape(DEPTH, n_pool, PAGE_SIZE * FX_HEADS, FX_HD)
    cv = cache_v.reshape(DEPTH, n_pool, PAGE_SIZE * FX_HEADS, FX_HD)

    xp = x_prompt.reshape(tp, d)
    xs = x_sample.reshape(ts, d)
    xp_b = xp.astype(BF16)
    xs_b = xs.astype(BF16)
    st_p, st_s = [], []
    for l in range(DEPTH):
        wl = _layer_weights(l, w_in, rw_mu, rw_w0, rw_w_up, rw_a0, rw_a_up, rw_g_up, rw_k_k, rw_k_a, rw_r_k,
                            rw_gn_g, rw_gn_b, hg_lb, hg_norm_g, fx_bf, p_rw, p_hg, p_fx, w_o, ln1_g, ln1_b,
                            ln2_g, ln2_b, w_router, b_router, w_up, b_up, w_down, b_down, consts)
        br_p, sp = _token_mixer(xp_b, wl, bp, lp_, jnp.zeros((bp, RW_COLS), F32),
                                jnp.zeros((bp, RW_HEADS, RW_HD, RW_HD), F32),
                                jnp.zeros((bp, HG_HEADS, HG_DK, HG_DV), F32), None, l, None)
        br_s, ss = _token_mixer(xs_b, wl, bs, ls_, state_shift[l], state_rwkv[l], state_hgrn[l],
                                (ck, cv, cache_logf), l, page_table)
        xp, xp_b = _mm_res_ln(_merge(*br_p, wl), wl["w_o"], xp, wl["ln1_g"], wl["ln1_b"])
        xs, xs_b = _mm_res_ln(_merge(*br_s, wl), wl["w_o"], xs, wl["ln1_g"], wl["ln1_b"])
        y = _moe([xp, xs], jnp.concatenate([xp_b, xs_b], axis=0), wl)
        xp, xp_b = _res_ln(y[:tp], xp, wl["ln2_g"], wl["ln2_b"])
        xs, xs_b = _res_ln(y[tp:], xs, wl["ln2_g"], wl["ln2_b"])
        st_p.append(sp)
        st_s.append(ss)
    stk = lambda sts, i: jnp.stack([s[i] for s in sts])
    return (xp.reshape(bp, lp_, d), xs.reshape(bs, ls_, d),
            stk(st_p, 0), stk(st_p, 1), stk(st_p, 2), stk(st_p, 3), stk(st_p, 4), stk(st_p, 5),
            stk(st_s, 0), stk(st_s, 1), stk(st_s, 2), stk(st_s, 3), stk(st_s, 4), stk(st_s, 5))
```
